```python
import math
import jax, jax.numpy as jnp
from jax import lax
import numpy as np

D_MODEL = 2048
BATCH = 4
SEQ = 4096
DEPTH = 1

CHUNK = 64
N_META = 16
Q_BLOCK = 128
EPS = 1e-6
NEG_BIG = -1e30

MLA_HEADS = 8
Q_LORA = 512
KV_LORA = 512
NOPE_DIM = 128
ROPE_DIM = 64
V_DIM = 128
ROPE_THETA = 10000.0

SB_HEADS = 8
SB_HEAD_DIM = 128

A_WIDTH = MLA_HEADS * V_DIM
B_WIDTH = SB_HEADS * SB_HEAD_DIM

IN_SPLITS = (Q_LORA, KV_LORA, ROPE_DIM, B_WIDTH, B_WIDTH, B_WIDTH, D_MODEL, D_MODEL)
IN_COLS = sum(IN_SPLITS)

PEER_HEADS = 8
PEER_NKEYS = 128
PEER_N_EXPERTS = PEER_NKEYS * PEER_NKEYS
PEER_TOPK = 16
PEER_QDIM = 256
PEER_HALF = PEER_QDIM // 2
PEER_BLOCK = 128

kernel_name = "hybrid_mla_stickbreaking_peer_block"


def _rmsnorm(x, g):
    xf = x.astype(jnp.float32)
    out = xf * lax.rsqrt(jnp.mean(xf * xf, axis=-1, keepdims=True) + EPS)
    return (out * g.astype(jnp.float32)).astype(x.dtype)


def _rope(x, cos, sin):
    xf = x.astype(jnp.float32)
    x1, x2 = xf[..., : ROPE_DIM // 2], xf[..., ROPE_DIM // 2:]
    return jnp.concatenate([x1 * cos - x2 * sin, x1 * sin + x2 * cos], axis=-1).astype(x.dtype)


def _to_blocks(a, nb):
    return jnp.moveaxis(a.reshape(a.shape[0], nb, Q_BLOCK, *a.shape[2:]), 1, 0)


def _from_blocks(a):
    a = jnp.moveaxis(a, 0, 1)
    return a.reshape(a.shape[0], a.shape[1] * a.shape[2], *a.shape[3:])


def _mla_attention(q_nope, q_rope, k_nope, k_rope, v, chunk_id):
    nb = q_nope.shape[1] // Q_BLOCK
    scale = 1.0 / math.sqrt(NOPE_DIM + ROPE_DIM)

    def block(args):
        qn, qr, cq = args
        s = (jnp.einsum('bqhd,bkhd->bhqk', qn, k_nope)
             + jnp.einsum('bqhr,bkr->bhqk', qr, k_rope)).astype(jnp.float32) * scale
        visible = cq[:, None] >= chunk_id[None, :]
        s = jnp.where(visible, s, NEG_BIG)
        p = jax.nn.softmax(s, axis=-1).astype(v.dtype)
        return jnp.einsum('bhqk,bkhd->bqhd', p, v)

    out = lax.map(block, (_to_blocks(q_nope, nb), _to_blocks(q_rope, nb),
                          chunk_id.reshape(nb, Q_BLOCK)))
    return _from_blocks(out)


def _stick_breaking_attention(q, k, v, pos):
    nb = q.shape[1] // Q_BLOCK
    scale = 1.0 / math.sqrt(SB_HEAD_DIM)

    def block(args):
        qb, pq = args
        z = jnp.einsum('bqhd,bkhd->bhqk', qb, k).astype(jnp.float32) * scale
        strict = pq[:, None] > pos[None, :]
        log_beta = jax.nn.log_sigmoid(z)
        log_1m = jnp.where(strict, jax.nn.log_sigmoid(-z), 0.0)
        cs = jnp.cumsum(log_1m, axis=-1)
        rest = cs[..., -1:] - cs
        a = jnp.where(strict, jnp.exp(log_beta + rest), 0.0).astype(v.dtype)
        return jnp.einsum('bhqk,bkhd->bqhd', a, v)

    out = lax.map(block, (_to_blocks(q, nb), pos.reshape(nb, Q_BLOCK)))
    return _from_blocks(out)


def _peer(xn, w_q, sub_keys, u, v):
    b, lp, d = xn.shape
    n_tok = b * lp
    xt = xn.reshape(n_tok // PEER_BLOCK, PEER_BLOCK, d)

    def block(xb):
        q = (xb @ w_q).reshape(PEER_BLOCK, PEER_HEADS, 2, PEER_HALF)
        s = jnp.einsum('thcd,hcnd->thcn', q, sub_keys).astype(jnp.float32)
        s1, i1 = lax.top_k(s[:, :, 0], PEER_TOPK)
        s2, i2 = lax.top_k(s[:, :, 1], PEER_TOPK)
        cand_s = (s1[..., :, None] + s2[..., None, :]).reshape(PEER_BLOCK, PEER_HEADS, -1)
        cand_i = (i1[..., :, None] * PEER_NKEYS + i2[..., None, :]).reshape(PEER_BLOCK, PEER_HEADS, -1)
        top_s, top_pos = lax.top_k(cand_s, PEER_TOPK)
        idx = jnp.take_along_axis(cand_i, top_pos, axis=-1)
        g = jax.nn.softmax(top_s, axis=-1).reshape(PEER_BLOCK, -1)
        idx = idx.reshape(PEER_BLOCK, -1)
        h = jnp.einsum('tkd,td->tk', u[idx], xb)
        act = (g * jax.nn.gelu(h.astype(jnp.float32), approximate=False)).astype(xb.dtype)
        return jnp.einsum('tk,tkd->td', act, v[idx])

    return lax.map(block, xt).reshape(b, lp, d)


def setup_inputs(seed: int = 0) -> dict:
    key = jax.random.key(seed)
    ks = jax.random.split(key, 20)
    f32 = jnp.float32

    def nrm(k, shape, scale):
        return jax.random.normal(k, shape, f32) * scale

    def gain(k, shape):
        return 1.0 + 0.05 * jax.random.normal(k, shape, f32)

    offsets = jax.random.randint(ks[1], (BATCH, 1), 0, 64) * CHUNK
    positions = (offsets + jnp.arange(SEQ, dtype=jnp.int32)[None, :]).astype(jnp.int32)
    return {
        "x": nrm(ks[0], (BATCH, SEQ, D_MODEL), 1.0),
        "positions": positions,
        "meta_tokens": nrm(ks[2], (N_META, D_MODEL), 1.0),
        "norm_mix_g": gain(ks[3], (DEPTH, D_MODEL)),
        "w_in": nrm(ks[4], (DEPTH, D_MODEL, IN_COLS), D_MODEL ** -0.5),
        "mla_q_norm_g": gain(ks[5], (DEPTH, Q_LORA)),
        "mla_w_uq": nrm(ks[6], (DEPTH, Q_LORA, MLA_HEADS * (NOPE_DIM + ROPE_DIM)), Q_LORA ** -0.5),
        "mla_kv_norm_g": gain(ks[7], (DEPTH, KV_LORA)),
        "mla_w_uk": nrm(ks[8], (DEPTH, KV_LORA, MLA_HEADS * NOPE_DIM), KV_LORA ** -0.5),
        "mla_w_uv": nrm(ks[9], (DEPTH, KV_LORA, MLA_HEADS * V_DIM), KV_LORA ** -0.5),
        "w_branch_a": nrm(ks[10], (DEPTH, A_WIDTH, D_MODEL), A_WIDTH ** -0.5),
        "w_branch_b": nrm(ks[11], (DEPTH, B_WIDTH, D_MODEL), B_WIDTH ** -0.5),
        "w_out": nrm(ks[12], (DEPTH, D_MODEL, D_MODEL), D_MODEL ** -0.5),
        "norm_ffn_g": gain(ks[13], (DEPTH, D_MODEL)),
        "peer_w_q": nrm(ks[14], (DEPTH, D_MODEL, PEER_HEADS * PEER_QDIM), D_MODEL ** -0.5),
        "peer_sub_keys": nrm(ks[15], (DEPTH, PEER_HEADS, 2, PEER_NKEYS, PEER_HALF), PEER_HALF ** -0.5),
        "peer_u": nrm(ks[16], (DEPTH, PEER_N_EXPERTS, D_MODEL), D_MODEL ** -0.5),
        "peer_v": nrm(ks[17], (DEPTH, PEER_N_EXPERTS, D_MODEL), 0.2),
        "final_norm_g": gain(ks[18], (D_MODEL,)),
    }


def reference(x, positions, meta_tokens, norm_mix_g, w_in, mla_q_norm_g, mla_w_uq,
              mla_kv_norm_g, mla_w_uk, mla_w_uv, w_branch_a, w_branch_b, w_out,
              norm_ffn_g, peer_w_q, peer_sub_keys, peer_u, peer_v, final_norm_g):
    b, seq, d = x.shape
    length = seq + N_META
    lp = ((length + Q_BLOCK - 1) // Q_BLOCK) * Q_BLOCK
    n_pad = lp - length

    meta = jnp.broadcast_to(meta_tokens.astype(x.dtype)[None], (b, N_META, d))
    h = jnp.concatenate([meta, x, jnp.zeros((b, n_pad, d), x.dtype)], axis=1)

    idx = jnp.arange(lp, dtype=jnp.int32)
    chunk_id = jnp.where(idx < N_META, 0, 1 + (idx - N_META) // CHUNK)

    rot_pos = jnp.concatenate([
        jnp.broadcast_to(jnp.arange(N_META, dtype=jnp.int32)[None], (b, N_META)),
        N_META + positions,
        jnp.zeros((b, n_pad), jnp.int32)], axis=1)
    inv_freq = ROPE_THETA ** (-jnp.arange(ROPE_DIM // 2, dtype=jnp.float32) / (ROPE_DIM // 2))
    ang = rot_pos.astype(jnp.float32)[..., None] * inv_freq
    cos, sin = jnp.cos(ang), jnp.sin(ang)

    split_at = list(np.cumsum(IN_SPLITS)[:-1])
    for l in range(DEPTH):
        hn = _rmsnorm(h, norm_mix_g[l])
        proj = hn @ w_in[l]
        c_q, c_kv, k_rope, q_sb, k_sb, v_sb, gate_a, gate_b = jnp.split(proj, split_at, axis=-1)

        q = (_rmsnorm(c_q, mla_q_norm_g[l]) @ mla_w_uq[l]).reshape(b, lp, MLA_HEADS, NOPE_DIM + ROPE_DIM)
        q_nope = q[..., :NOPE_DIM]
        q_rope = _rope(q[..., NOPE_DIM:], cos[:, :, None], sin[:, :, None])
        ckv = _rmsnorm(c_kv, mla_kv_norm_g[l])
        k_nope = (ckv @ mla_w_uk[l]).reshape(b, lp, MLA_HEADS, NOPE_DIM)
        v_a = (ckv @ mla_w_uv[l]).reshape(b, lp, MLA_HEADS, V_DIM)
        k_rope = _rope(k_rope, cos, sin)
        y_a = _mla_attention(q_nope, q_rope, k_nope, k_rope, v_a, chunk_id).reshape(b, lp, A_WIDTH)

        y_b = _stick_breaking_attention(
            q_sb.reshape(b, lp, SB_HEADS, SB_HEAD_DIM),
            k_sb.reshape(b, lp, SB_HEADS, SB_HEAD_DIM),
            v_sb.reshape(b, lp, SB_HEADS, SB_HEAD_DIM), idx).reshape(b, lp, B_WIDTH)

        merged = (jax.nn.sigmoid(gate_a) * (y_a @ w_branch_a[l])
                  + jax.nn.sigmoid(gate_b) * (y_b @ w_branch_b[l]))
        h = h + merged @ w_out[l]

        h = h + _peer(_rmsnorm(h, norm_ffn_g[l]), peer_w_q[l], peer_sub_keys[l], peer_u[l], peer_v[l])

    h = _rmsnorm(h, final_norm_g)
    return h[:, N_META:N_META + seq]
```

```python
import functools
import math

import jax
import jax.numpy as jnp
from jax import lax
from jax.experimental import pallas as pl
from jax.experimental.pallas import tpu as pltpu

F32 = jnp.float32
BF16 = jnp.bfloat16

CHUNK = 64
N_META = 16
EPS = 1e-6
NEG_BIG = -1e30

MLA_HEADS = 8
Q_LORA = 512
KV_LORA = 512
NOPE_DIM = 128
ROPE_DIM = 64
V_DIM = 128
ROPE_THETA = 10000.0
SB_HEADS = 8
SB_HEAD_DIM = 128
A_WIDTH = MLA_HEADS * V_DIM
B_WIDTH = SB_HEADS * SB_HEAD_DIM

PEER_HEADS = 8
PEER_NKEYS = 128
PEER_TOPK = 16
PEER_HALF = 128

LANE = 128
MLA_QK = 2 * LANE
META_PAD = LANE
VMEM_LIMIT = 56 * 1024 * 1024

COL_CQ, COL_CKV, COL_QSB, COL_KSB, COL_VSB = 0, 512, 1024, 2048, 3072
COL_GA, COL_GB, COL_KR, COL_KROT = 4096, 6144, 8192, 8320
PROJ_COLS = 8448


def _params(sem):
    return pltpu.CompilerParams(dimension_semantics=sem, vmem_limit_bytes=VMEM_LIMIT)


def _dot(a, b):
    return jnp.dot(a, b, preferred_element_type=F32)


def _dot_nt(a, b):
    return lax.dot_general(a, b, (((1,), (1,)), ((), ())), preferred_element_type=F32)


def _gelu(x):
    return 0.5 * x * (1.0 + lax.erf(x * math.sqrt(0.5)))


def _rms(x, g):
    return x * lax.rsqrt(jnp.mean(x * x, axis=-1, keepdims=True) + EPS) * g


def _inproj_kernel(x_ref, g_ref, w_ref, o_ref, xn_ref):
    @pl.when(pl.program_id(1) == 0)
    def _():
        xn_ref[...] = _rms(x_ref[...], g_ref[...]).astype(BF16)

    o_ref[...] = _dot(xn_ref[...], w_ref[...]).astype(o_ref.dtype)


def _inproj(x2d, g, w, tm, tn):
    n, d = x2d.shape
    cols = w.shape[1]
    return pl.pallas_call(
        _inproj_kernel,
        out_shape=jax.ShapeDtypeStruct((n, cols), BF16),
        grid=(n // tm, cols // tn),
        in_specs=[pl.BlockSpec((tm, d), lambda i, j: (i, 0)),
                  pl.BlockSpec((1, d), lambda i, j: (0, 0)),
                  pl.BlockSpec((d, tn), lambda i, j: (0, j))],
        out_specs=pl.BlockSpec((tm, tn), lambda i, j: (i, j)),
        scratch_shapes=[pltpu.VMEM((tm, d), BF16)],
        compiler_params=_params(("parallel", "arbitrary")),
        name="inproj",
    )(x2d, g, w)


def _mla_proj_kernel(c_ref, kr_ref, ang_ref, gq_ref, gkv_ref, wq_ref, wqrot_ref, wuk_ref, wuv_ref,
                     q_ref, k_ref, v_ref):
    scale = 1.0 / math.sqrt(NOPE_DIM + ROPE_DIM)
    ang = ang_ref[...]
    cos, sin = jnp.cos(ang), jnp.sin(ang)
    c = c_ref[...].astype(F32)
    cqn = _rms(c[:, :Q_LORA], gq_ref[...]).astype(BF16)
    ckvn = _rms(c[:, Q_LORA:], gkv_ref[...]).astype(BF16)
    q = _dot(cqn, wq_ref[...])
    qrot = _dot(cqn, wqrot_ref[...])
    kn = _dot(ckvn, wuk_ref[...])
    v_ref[...] = _dot(ckvn, wuv_ref[...]).astype(BF16)
    kr = kr_ref[...].astype(F32)
    k_rope = (kr[:, :LANE] * cos + kr[:, LANE:] * sin).astype(BF16)
    for h in range(MLA_HEADS):
        base = h * MLA_QK
        q_ref[:, base:base + LANE] = (q[:, base:base + LANE] * scale).astype(BF16)
        rope = q[:, base + LANE:base + MLA_QK] * cos + qrot[:, h * LANE:(h + 1) * LANE] * sin
        q_ref[:, base + LANE:base + MLA_QK] = (rope * scale).astype(BF16)
        k_ref[:, base:base + LANE] = kn[:, h * LANE:(h + 1) * LANE].astype(BF16)
        k_ref[:, base + LANE:base + MLA_QK] = k_rope


def _mla_proj(proj, ang, gq, gkv, wq, wqrot, wuk, wuv, tm):
    n = proj.shape[0]
    const = lambda i: (0, 0)
    return pl.pallas_call(
        _mla_proj_kernel,
        out_shape=(jax.ShapeDtypeStruct((n, MLA_HEADS * MLA_QK), BF16),
                   jax.ShapeDtypeStruct((n, MLA_HEADS * MLA_QK), BF16),
                   jax.ShapeDtypeStruct((n, A_WIDTH), BF16)),
        grid=(n // tm,),
        in_specs=[pl.BlockSpec((tm, Q_LORA + KV_LORA), lambda i: (i, 0)),
                  pl.BlockSpec((tm, 2 * LANE), lambda i: (i, COL_KR // (2 * LANE))),
                  pl.BlockSpec((tm, LANE), lambda i: (i, 0)),
                  pl.BlockSpec((1, Q_LORA), const),
                  pl.BlockSpec((1, KV_LORA), const),
                  pl.BlockSpec(wq.shape, const),
                  pl.BlockSpec(wqrot.shape, const),
                  pl.BlockSpec(wuk.shape, const),
                  pl.BlockSpec(wuv.shape, const)],
        out_specs=(pl.BlockSpec((tm, MLA_HEADS * MLA_QK), lambda i: (i, 0)),
                   pl.BlockSpec((tm, MLA_HEADS * MLA_QK), lambda i: (i, 0)),
                   pl.BlockSpec((tm, A_WIDTH), lambda i: (i, 0))),
        compiler_params=_params(("parallel",)),
        name="mla_proj",
    )(proj, proj, ang, gq, gkv, wq, wqrot, wuk, wuv)


def _mla_attn_kernel(q_ref, k_ref, v_ref, km_ref, vm_ref, o_ref, *, tq):
    qi = pl.program_id(2)
    q = q_ref[...]

    s = _dot_nt(q, km_ref[...])
    col = lax.broadcasted_iota(jnp.int32, s.shape, 1)
    s = jnp.where(col < N_META, s, NEG_BIG)
    m = jnp.max(s, axis=-1, keepdims=True)
    p = jnp.exp(s - m)
    l = jnp.sum(p, axis=-1, keepdims=True)
    acc = _dot(p.astype(BF16), vm_ref[...])

    def step(j, carry, masked):
        m, l, acc = carry
        start = pl.multiple_of(j * tq, tq)
        s = _dot_nt(q, k_ref[pl.ds(start, tq), :])
        if masked:
            row = lax.broadcasted_iota(jnp.int32, s.shape, 0) // CHUNK
            colc = lax.broadcasted_iota(jnp.int32, s.shape, 1) // CHUNK
            s = jnp.where(row >= colc, s, NEG_BIG)
        m_new = jnp.maximum(m, jnp.max(s, axis=-1, keepdims=True))
        alpha = jnp.exp(m - m_new)
        p = jnp.exp(s - m_new)
        l = alpha * l + jnp.sum(p, axis=-1, keepdims=True)
        acc = alpha * acc + _dot(p.astype(BF16), v_ref[pl.ds(start, tq), :])
        return m_new, l, acc

    carry = lax.fori_loop(0, qi, functools.partial(step, masked=False), (m, l, acc))
    m, l, acc = step(qi, carry, True)
    o_ref[...] = (acc / l).astype(o_ref.dtype)


def _mla_attn(q, k, v, km, vm, batch, seq, tq):
    nq = seq // tq
    return pl.pallas_call(
        functools.partial(_mla_attn_kernel, tq=tq),
        out_shape=jax.ShapeDtypeStruct((batch * seq, A_WIDTH), BF16),
        grid=(batch, MLA_HEADS, nq),
        in_specs=[pl.BlockSpec((tq, MLA_QK), lambda b, h, i: (b * nq + i, h)),
                  pl.BlockSpec((seq, MLA_QK), lambda b, h, i: (b, h)),
                  pl.BlockSpec((seq, V_DIM), lambda b, h, i: (b, h)),
                  pl.BlockSpec((META_PAD, MLA_QK), lambda b, h, i: (0, h)),
                  pl.BlockSpec((META_PAD, V_DIM), lambda b, h, i: (0, h))],
        out_specs=pl.BlockSpec((tq, V_DIM), lambda b, h, i: (b * nq + i, h)),
        compiler_params=_params(("parallel", "parallel", "arbitrary")),
        name="mla_attn",
    )(q, k, v, km, vm)


def _sb_attn_kernel(q_ref, k_ref, v_ref, km_ref, vm_ref, o_ref, *, tq):
    qi = pl.program_id(2)
    scale = 1.0 / math.sqrt(SB_HEAD_DIM)
    q = (q_ref[...].astype(F32) * scale).astype(BF16)

    def tile(k, v, valid, r_sum, acc):
        width = k.shape[0]
        z = _dot_nt(q, k)
        log_beta = jnp.minimum(z, 0.0) - jnp.log1p(jnp.exp(-jnp.abs(z)))
        log_1m = log_beta - z
        if valid is not None:
            log_1m = jnp.where(valid, log_1m, 0.0)
        src = lax.broadcasted_iota(jnp.int32, (width, width), 0)
        dst = lax.broadcasted_iota(jnp.int32, (width, width), 1)
        tri = jnp.where(src > dst, 1.0, 0.0).astype(BF16)
        hi = log_1m.astype(BF16)
        lo = (log_1m - hi.astype(F32)).astype(BF16)
        rest = _dot(hi, tri) + _dot(lo, tri) + r_sum
        a = jnp.exp(log_beta + rest)
        if valid is not None:
            a = jnp.where(valid, a, 0.0)
        acc = acc + _dot(a.astype(BF16), v)
        r_sum = r_sum + jnp.sum(log_1m, axis=-1, keepdims=True)
        return r_sum, acc

    start = pl.multiple_of(qi * tq, tq)
    row = lax.broadcasted_iota(jnp.int32, (tq, tq), 0)
    colm = lax.broadcasted_iota(jnp.int32, (tq, tq), 1)
    r_sum = jnp.zeros((tq, 1), F32)
    acc = jnp.zeros((tq, SB_HEAD_DIM), F32)
    r_sum, acc = tile(k_ref[pl.ds(start, tq), :], v_ref[pl.ds(start, tq), :], row > colm, r_sum, acc)

    def body(jj, carry):
        st = pl.multiple_of((qi - 1 - jj) * tq, tq)
        return tile(k_ref[pl.ds(st, tq), :], v_ref[pl.ds(st, tq), :], None, *carry)

    r_sum, acc = lax.fori_loop(0, qi, body, (r_sum, acc))

    mcol = lax.broadcasted_iota(jnp.int32, (tq, META_PAD), 1)
    r_sum, acc = tile(km_ref[...], vm_ref[...], mcol < N_META, r_sum, acc)
    o_ref[...] = acc.astype(o_ref.dtype)


def _sb_attn(proj, proj_meta, batch, seq, tq):
    nq = seq // tq
    qc, kc, vc = COL_QSB // LANE, COL_KSB // LANE, COL_VSB // LANE
    return pl.pallas_call(
        functools.partial(_sb_attn_kernel, tq=tq),
        out_shape=jax.ShapeDtypeStruct((batch * seq, B_WIDTH), BF16),
        grid=(batch, SB_HEADS, nq),
        in_specs=[pl.BlockSpec((tq, LANE), lambda b, h, i: (b * nq + i, qc + h)),
                  pl.BlockSpec((seq, LANE), lambda b, h, i: (b, kc + h)),
                  pl.BlockSpec((seq, LANE), lambda b, h, i: (b, vc + h)),
                  pl.BlockSpec((META_PAD, LANE), lambda b, h, i: (0, kc + h)),
                  pl.BlockSpec((META_PAD, LANE), lambda b, h, i: (0, vc + h))],
        out_specs=pl.BlockSpec((tq, LANE), lambda b, h, i: (b * nq + i, h)),
        compiler_params=_params(("parallel", "parallel", "arbitrary")),
        name="sb_attn",
    )(proj, proj, proj, proj_meta, proj_meta)


def _merge_kernel(x_ref, ya_ref, yb_ref, ga_ref, gb_ref, wa_ref, wb_ref, wo_ref, o_ref):
    ga = jax.nn.sigmoid(ga_ref[...].astype(F32))
    gb = jax.nn.sigmoid(gb_ref[...].astype(F32))
    merged = ga * _dot(ya_ref[...], wa_ref[...]) + gb * _dot(yb_ref[...], wb_ref[...])
    o_ref[...] = x_ref[...] + _dot(merged.astype(BF16), wo_ref[...])


def _merge(x2d, ya, yb, proj, wa, wb, wo, tm):
    n, d = x2d.shape
    const = lambda i: (0, 0)
    single = pl.Buffered(1)
    return pl.pallas_call(
        _merge_kernel,
        out_shape=jax.ShapeDtypeStruct((n, d), F32),
        grid=(n // tm,),
        in_specs=[pl.BlockSpec((tm, d), lambda i: (i, 0)),
                  pl.BlockSpec((tm, A_WIDTH), lambda i: (i, 0)),
                  pl.BlockSpec((tm, B_WIDTH), lambda i: (i, 0)),
                  pl.BlockSpec((tm, d), lambda i: (i, COL_GA // d)),
                  pl.BlockSpec((tm, d), lambda i: (i, COL_GB // d)),
                  pl.BlockSpec(wa.shape, const, pipeline_mode=single),
                  pl.BlockSpec(wb.shape, const, pipeline_mode=single),
                  pl.BlockSpec(wo.shape, const, pipeline_mode=single)],
        out_specs=pl.BlockSpec((tm, d), lambda i: (i, 0)),
        compiler_params=_params(("parallel",)),
        name="merge_out",
    )(x2d, ya, yb, proj, proj, wa, wb, wo)


def _staircase_pairs():
    return [(i, j) for i in range(PEER_TOPK) for j in range(PEER_TOPK) if (i + 1) * (j + 1) <= PEER_TOPK]


def _peer_route_kernel(h_ref, g_ref, wq_ref, keys_ref, xnt_ref, s1_ref, s2_ref, e1_ref, e2_ref, tau_ref,
                       top_ref):
    xn = _rms(h_ref[...], g_ref[...])
    xnt = xn.T.astype(BF16)
    xnt_ref[...] = xnt
    qt = _dot(wq_ref[...], xnt).astype(BF16)

    neg_inf = jnp.float32(-jnp.inf)
    for c, s_ref in enumerate((s1_ref, s2_ref)):
        for h in range(PEER_HEADS):
            r0 = (c * PEER_HEADS + h) * PEER_HALF
            s = _dot(keys_ref[h, c], qt[r0:r0 + PEER_HALF, :])
            s_ref[h] = s
            w = s
            for r in range(PEER_TOPK):
                mx = jnp.max(w, axis=0, keepdims=True)
                top_ref[c, r, h:h + 1, :] = mx
                w = jnp.where(w == mx, neg_inf, w)

    v1 = [top_ref[0, r] for r in range(PEER_TOPK)]
    v2 = [top_ref[1, r] for r in range(PEER_TOPK)]
    cands = [v1[i] + v2[j] for i, j in _staircase_pairs()]
    tau = jnp.full_like(cands[0], neg_inf)
    for ci in cands:
        cnt = jnp.zeros_like(ci)
        for cj in cands:
            cnt = cnt + jnp.where(cj >= ci, 1.0, 0.0)
        tau = jnp.maximum(tau, jnp.where(cnt >= PEER_TOPK, ci, neg_inf))
    m1, m2 = v1[0], v2[0]
    top = m1 + m2
    z = jnp.zeros_like(tau)
    for ci in cands:
        z = z + jnp.where(ci >= tau, jnp.exp(ci - top), 0.0)
    inv_z = 1.0 / z
    tau_ref[...] = tau
    for h in range(PEER_HEADS):
        e1_ref[h] = jnp.exp(s1_ref[h] - m1[h:h + 1, :]) * inv_z[h:h + 1, :]
        e2_ref[h] = jnp.exp(s2_ref[h] - m2[h:h + 1, :])


def _peer_route(h1, g, wqt, keys, tt):
    n, d = h1.shape
    const2 = lambda i: (0, 0)
    hk = (PEER_HEADS, PEER_NKEYS, n)
    blk = pl.BlockSpec((PEER_HEADS, PEER_NKEYS, tt), lambda i: (0, 0, i))
    return pl.pallas_call(
        _peer_route_kernel,
        out_shape=(jax.ShapeDtypeStruct((d, n), BF16),
                   jax.ShapeDtypeStruct(hk, F32), jax.ShapeDtypeStruct(hk, F32),
                   jax.ShapeDtypeStruct(hk, F32), jax.ShapeDtypeStruct(hk, F32),
                   jax.ShapeDtypeStruct((PEER_HEADS, n), F32)),
        grid=(n // tt,),
        in_specs=[pl.BlockSpec((tt, d), lambda i: (i, 0)),
                  pl.BlockSpec((1, d), const2),
                  pl.BlockSpec(wqt.shape, const2, pipeline_mode=pl.Buffered(1)),
                  pl.BlockSpec(keys.shape, lambda i: (0, 0, 0, 0))],
        out_specs=(pl.BlockSpec((d, tt), lambda i: (0, i)), blk, blk, blk, blk,
                   pl.BlockSpec((PEER_HEADS, tt), lambda i: (0, i))),
        scratch_shapes=[pltpu.VMEM((2, PEER_TOPK, PEER_HEADS, tt), F32)],
        compiler_params=_params(("parallel",)),
        name="peer_route",
    )(h1, g, wqt, keys)


def _peer_expert_kernel(xnt_ref, u_ref, vt_ref, s1_ref, e1_ref, s2_ref, e2_ref, tau_ref, o_ref,
                        acc_ref, act_ref, *, groups):
    j = pl.program_id(1)

    @pl.when(j == 0)
    def _():
        acc_ref[...] = jnp.zeros_like(acc_ref)

    hid = _dot(u_ref[...], xnt_ref[...])
    for al in range(groups):
        gate = jnp.zeros((PEER_NKEYS, hid.shape[1]), F32)
        for h in range(PEER_HEADS):
            chosen = (s1_ref[h, al:al + 1, :] + s2_ref[h]) >= tau_ref[h:h + 1, :]
            gate = gate + jnp.where(chosen, e2_ref[h] * e1_ref[h, al:al + 1, :], 0.0)
        rows = slice(al * PEER_NKEYS, (al + 1) * PEER_NKEYS)
        act_ref[rows, :] = (gate * _gelu(hid[rows, :])).astype(BF16)
    acc_ref[...] += _dot(vt_ref[...], act_ref[...])

    @pl.when(j == pl.num_programs(1) - 1)
    def _():
        o_ref[...] = acc_ref[...].T


def _peer_experts(xnt, u, vt, s1, e1, s2, e2, tau, tt, groups):
    d, n = xnt.shape
    n_exp = u.shape[0]
    te = groups * PEER_NKEYS
    single = pl.Buffered(1)
    sel = pl.BlockSpec((PEER_HEADS, groups, tt), lambda i, j: (0, j, i))
    full = pl.BlockSpec((PEER_HEADS, PEER_NKEYS, tt), lambda i, j: (0, 0, i), pipeline_mode=single)
    return pl.pallas_call(
        functools.partial(_peer_expert_kernel, groups=groups),
        out_shape=jax.ShapeDtypeStruct((n, d), F32),
        grid=(n // tt, n_exp // te),
        in_specs=[pl.BlockSpec((d, tt), lambda i, j: (0, i), pipeline_mode=single),
                  pl.BlockSpec((te, d), lambda i, j: (j, 0)),
                  pl.BlockSpec((d, te), lambda i, j: (0, j)),
                  sel, sel, full, full,
                  pl.BlockSpec((PEER_HEADS, tt), lambda i, j: (0, i))],
        out_specs=pl.BlockSpec((tt, d), lambda i, j: (i, 0)),
        scratch_shapes=[pltpu.VMEM((d, tt), F32), pltpu.VMEM((te, tt), BF16)],
        compiler_params=_params(("parallel", "arbitrary")),
        name="peer_experts",
    )(xnt, u, vt, s1, e1, s2, e2, tau)


def _final_kernel(h_ref, p_ref, g_ref, o_ref):
    o_ref[...] = _rms(h_ref[...] + p_ref[...], g_ref[...])


def _final(h1, peer, g, tm):
    n, d = h1.shape
    row = pl.BlockSpec((tm, d), lambda i: (i, 0))
    return pl.pallas_call(
        _final_kernel,
        out_shape=jax.ShapeDtypeStruct((n, d), F32),
        grid=(n // tm,),
        in_specs=[row, row, pl.BlockSpec((1, d), lambda i: (0, 0))],
        out_specs=row,
        compiler_params=_params(("parallel",)),
        name="final_norm",
    )(h1, peer, g)


def _rot_half(w):
    half = ROPE_DIM // 2
    return jnp.concatenate([-w[..., half:], w[..., :half]], axis=-1)


def _pad_lanes(w):
    return jnp.pad(w, [(0, 0)] * (w.ndim - 1) + [(0, LANE - w.shape[-1])])


def _prep_w_in(w_in):
    cq, ckv, kr, qsb, ksb, vsb, ga, gb = jnp.split(
        w_in, [512, 1024, 1088, 2112, 3136, 4160, 6208], axis=-1)
    return jnp.concatenate([cq, ckv, qsb, ksb, vsb, ga, gb, _pad_lanes(kr), _pad_lanes(_rot_half(kr))],
                           axis=-1).astype(BF16)


def _prep_w_uq(w_uq):
    w = w_uq.reshape(Q_LORA, MLA_HEADS, NOPE_DIM + ROPE_DIM)
    nope, rope = w[..., :NOPE_DIM], w[..., NOPE_DIM:]
    wq = jnp.concatenate([nope, _pad_lanes(rope)], axis=-1).reshape(Q_LORA, MLA_HEADS * MLA_QK)
    wqrot = _pad_lanes(_rot_half(rope)).reshape(Q_LORA, MLA_HEADS * LANE)
    return wq.astype(BF16), wqrot.astype(BF16)


def _angles(rot_pos):
    inv_freq = ROPE_THETA ** (-jnp.arange(ROPE_DIM // 2, dtype=F32) / (ROPE_DIM // 2))
    ang = rot_pos.astype(F32)[:, None] * inv_freq
    return _pad_lanes(jnp.concatenate([ang, ang], axis=-1))


def _pad_rows(a, rows):
    return jnp.pad(a, ((0, rows - a.shape[0]), (0, 0)))


def kernel(x, positions, meta_tokens, norm_mix_g, w_in, mla_q_norm_g, mla_w_uq, mla_kv_norm_g, mla_w_uk,
           mla_w_uv, w_branch_a, w_branch_b, w_out, norm_ffn_g, peer_w_q, peer_sub_keys, peer_u, peer_v,
           final_norm_g):
    batch, seq, d = x.shape
    n = batch * seq
    depth = w_in.shape[0]
    assert depth == 1, "meta/padding rows are only droppable for a single layer"
    tq = 256
    assert seq % tq == 0 and n % 1024 == 0

    x2d = x.reshape(n, d)
    meta = meta_tokens.astype(x.dtype)
    row = lambda g: g.reshape(1, -1)

    w_in_p = _prep_w_in(w_in[0])
    wq, wqrot = _prep_w_uq(mla_w_uq[0])
    wuk, wuv = mla_w_uk[0].astype(BF16), mla_w_uv[0].astype(BF16)
    gq, gkv = row(mla_q_norm_g[0]), row(mla_kv_norm_g[0])

    proj = _inproj(x2d, row(norm_mix_g[0]), w_in_p, 1024, 768)
    proj_m = _inproj(meta, row(norm_mix_g[0]), w_in_p, N_META, 768)

    ang = _angles((N_META + positions).reshape(n))
    ang_m = _angles(jnp.arange(N_META, dtype=jnp.int32))
    q_a, k_a, v_a = _mla_proj(proj, ang, gq, gkv, wq, wqrot, wuk, wuv, 512)
    _, k_m, v_m = _mla_proj(proj_m, ang_m, gq, gkv, wq, wqrot, wuk, wuv, N_META)

    y_a = _mla_attn(q_a, k_a, v_a, _pad_rows(k_m, META_PAD), _pad_rows(v_m, META_PAD), batch, seq, tq)
    y_b = _sb_attn(proj, _pad_rows(proj_m, META_PAD), batch, seq, tq)

    h1 = _merge(x2d, y_a, y_b, proj, w_branch_a[0].astype(BF16), w_branch_b[0].astype(BF16),
                w_out[0].astype(BF16), 512)

    wqt = peer_w_q[0].reshape(d, PEER_HEADS, 2, PEER_HALF).transpose(2, 1, 3, 0).reshape(-1, d).astype(BF16)
    keys = peer_sub_keys[0].astype(BF16)
    xnt, s1, s2, e1, e2, tau = _peer_route(h1, row(norm_ffn_g[0]), wqt, keys, 512)
    peer = _peer_experts(xnt, peer_u[0].astype(BF16), peer_v[0].T.astype(BF16), s1, e1, s2, e2, tau, 512, 8)

    out = _final(h1, peer, row(final_norm_g), 512)
    return out.reshape(batch, seq, d)
```
